```python
import math
import jax, jax.numpy as jnp
from jax import lax
import numpy as np

D_MODEL = 1024
BATCH = 8
SEQ = 2048
DEPTH = 2

F32 = jnp.float32
EPS = 1e-6
NEG_INF = -1e30

N_BRANCH = 4
BRANCH_W = D_MODEL // 2
HEAD_DIM = 64
Q_BLOCK = 128
DA_HEADS = BRANCH_W // (2 * HEAD_DIM)
CONV_W = 3
POOL_WINDOWS = (2, 4, 8, 16)
N_POOL = len(POOL_WINDOWS)
POOL_GROUP = BRANCH_W // N_POOL
SW_WINDOW = 128
SW_Q_HEADS = BRANCH_W // HEAD_DIM
SW_KV_HEADS = 2
SW_REP = SW_Q_HEADS // SW_KV_HEADS
PEER_HEADS = 8
PEER_NKEYS = 128
PEER_EXPERTS = PEER_NKEYS * PEER_NKEYS
PEER_DKEY = 256
PEER_TOPK = 16
PEER_CHUNK = 128

COL_SIZES = (
    DA_HEADS * 2 * HEAD_DIM, DA_HEADS * 2 * HEAD_DIM, DA_HEADS * 2 * HEAD_DIM,
    BRANCH_W, BRANCH_W, BRANCH_W,
    BRANCH_W,
    SW_Q_HEADS * HEAD_DIM, SW_KV_HEADS * HEAD_DIM, SW_KV_HEADS * HEAD_DIM,
    N_BRANCH * D_MODEL,
)
IN_COLS = sum(COL_SIZES)
SPLIT_POINTS = tuple(int(v) for v in np.cumsum(COL_SIZES)[:-1])

kernel_name = 'hybrid_gated_parallel_mixers_peer'


def rmsnorm(x, g):
    xf = x.astype(F32)
    y = xf * lax.rsqrt(jnp.mean(xf * xf, axis=-1, keepdims=True) + EPS)
    return (y * g.astype(F32)).astype(x.dtype)


def diff_attention(q, k, v, lam_q1, lam_k1, lam_q2, lam_k2, g_subln, lam_init):
    Bn, S = q.shape[0], q.shape[1]
    lam = (jnp.exp(jnp.sum(lam_q1.astype(F32) * lam_k1.astype(F32)))
           - jnp.exp(jnp.sum(lam_q2.astype(F32) * lam_k2.astype(F32))) + lam_init)
    scale = HEAD_DIM ** -0.5
    outs = []
    for i in range(S // Q_BLOCK):
        q0 = i * Q_BLOCK
        kend = q0 + Q_BLOCK
        s = jnp.einsum('bqhcd,bkhcd->bhcqk', q[:, q0:kend], k[:, :kend]).astype(F32) * scale
        causal = jnp.arange(kend)[None, :] <= (q0 + jnp.arange(Q_BLOCK))[:, None]
        p = jax.nn.softmax(jnp.where(causal, s, NEG_INF), axis=-1)
        attn = p[:, :, 0] - lam * p[:, :, 1]
        outs.append(jnp.einsum('bhqk,bkhe->bqhe', attn.astype(v.dtype), v[:, :kend]))
    o = jnp.concatenate(outs, axis=1)
    o = rmsnorm(o, g_subln) * (1.0 - lam_init)
    return o.reshape(Bn, S, DA_HEADS * 2 * HEAD_DIM)


def short_conv_mixer(gate_b, gate_c, xt, conv_w):
    S = xt.shape[1]
    z = gate_c * xt
    zp = jnp.pad(z, ((0, 0), (CONV_W - 1, 0), (0, 0)))
    y = sum(conv_w[j] * zp[:, j:j + S] for j in range(CONV_W))
    return gate_b * y


def multiscale_pool(z, w_pool, pool_scale):
    Bn, S, _ = z.shape
    zg = z.astype(F32).reshape(Bn, S, N_POOL, POOL_GROUP)
    cs = jnp.cumsum(zg, axis=1)
    pos1 = jnp.arange(1, S + 1)
    means = []
    for gi, w in enumerate(POOL_WINDOWS):
        cp = jnp.pad(cs[:, :, gi], ((0, 0), (w, 0), (0, 0)))
        win_sum = cp[:, w:] - cp[:, :S]
        count = jnp.minimum(pos1, w).astype(F32)
        means.append(win_sum / count[None, :, None])
    d = jnp.stack(means, axis=2) - zg
    y = jnp.einsum('bsgc,gce->bsge', d, w_pool.astype(F32))
    return (y.reshape(Bn, S, BRANCH_W) * pool_scale.astype(F32)).astype(z.dtype)


def sliding_window_sink_attention(q, k, v, sinks):
    Bn, S = q.shape[0], q.shape[1]
    nb = S // Q_BLOCK
    qb = q.reshape(Bn, nb, Q_BLOCK, SW_KV_HEADS, SW_REP, HEAD_DIM)

    def with_prev(t):
        t = t.reshape(Bn, nb, Q_BLOCK, SW_KV_HEADS, HEAD_DIM)
        prev = jnp.pad(t, ((0, 0), (1, 0), (0, 0), (0, 0), (0, 0)))[:, :nb]
        return jnp.concatenate([prev, t], axis=2)

    kk, vv = with_prev(k), with_prev(v)
    s = jnp.einsum('bnqgrd,bnkgd->bngrqk', qb, kk).astype(F32) * (HEAD_DIM ** -0.5)
    qi = jnp.arange(Q_BLOCK)[:, None]
    kj = jnp.arange(2 * Q_BLOCK)[None, :]
    rel = qi + Q_BLOCK - kj
    band = (rel >= 0) & (rel < SW_WINDOW)
    valid = band[None] & ((jnp.arange(nb)[:, None, None] > 0) | (kj >= Q_BLOCK)[None])
    s = jnp.where(valid[None, :, None, None], s, NEG_INF)
    sink = sinks.astype(F32).reshape(SW_KV_HEADS, SW_REP)[None, None, :, :, None, None]
    sink = jnp.broadcast_to(sink, s.shape[:-1] + (1,))
    p = jax.nn.softmax(jnp.concatenate([s, sink], axis=-1), axis=-1)[..., :-1]
    o = jnp.einsum('bngrqk,bnkgd->bnqgrd', p.astype(v.dtype), vv)
    return o.reshape(Bn, S, SW_Q_HEADS * HEAD_DIM)


def peer_ffn(h, w_q, sub_keys, u_experts, v_experts):
    Bn, S, D = h.shape
    T = Bn * S
    ht = h.reshape(T, D)
    q = (ht @ w_q).reshape(T, PEER_HEADS, 2, PEER_DKEY // 2)
    s = jnp.einsum('thpd,pnd->thpn', q, sub_keys).astype(F32)
    top_s, top_i = lax.top_k(s, PEER_TOPK)
    cand_s = (top_s[:, :, 0, :, None] + top_s[:, :, 1, None, :]).reshape(T, PEER_HEADS, -1)
    cand_i = (top_i[:, :, 0, :, None] * PEER_NKEYS + top_i[:, :, 1, None, :]).reshape(T, PEER_HEADS, -1)
    best_s, best_pos = lax.top_k(cand_s, PEER_TOPK)
    idx = jnp.take_along_axis(cand_i, best_pos, axis=-1)
    g = jax.nn.softmax(best_s, axis=-1)
    nchunk = T // PEER_CHUNK

    def expert_chunk(args):
        xc, ic, gc = args
        act = jax.nn.gelu(jnp.einsum('cd,chkd->chk', xc, u_experts[ic]).astype(F32), approximate=False)
        wgt = (gc * act).astype(xc.dtype)
        return jnp.einsum('chk,chkd->cd', wgt, v_experts[ic])

    out = lax.map(expert_chunk, (ht.reshape(nchunk, PEER_CHUNK, D),
                                 idx.reshape(nchunk, PEER_CHUNK, PEER_HEADS, PEER_TOPK),
                                 g.reshape(nchunk, PEER_CHUNK, PEER_HEADS, PEER_TOPK)))
    return out.reshape(Bn, S, D)


def hybrid_layer(x, mod, g_norm1, w_in, lam_q1, lam_k1, lam_q2, lam_k2, g_subln, conv_w,
                 w_pool, pool_scale, sinks, w_branch, w_o, g_norm2, w_q, sub_keys,
                 u_experts, v_experts, lam_init):
    Bn, S, _ = x.shape
    shift1, scale1, gate1, shift2, scale2, gate2 = [m[:, None, :] for m in jnp.split(mod, 6, axis=-1)]
    h = rmsnorm(x, g_norm1) * (1.0 + scale1) + shift1
    aq, ak, av, bb, bc, bx, cz, dq, dk, dv, gl = jnp.split(h @ w_in, SPLIT_POINTS, axis=-1)
    ya = diff_attention(aq.reshape(Bn, S, DA_HEADS, 2, HEAD_DIM),
                        ak.reshape(Bn, S, DA_HEADS, 2, HEAD_DIM),
                        av.reshape(Bn, S, DA_HEADS, 2 * HEAD_DIM),
                        lam_q1, lam_k1, lam_q2, lam_k2, g_subln, lam_init)
    yb = short_conv_mixer(bb, bc, bx, conv_w)
    yc = multiscale_pool(cz, w_pool, pool_scale)
    yd = sliding_window_sink_attention(dq.reshape(Bn, S, SW_Q_HEADS, HEAD_DIM),
                                       dk.reshape(Bn, S, SW_KV_HEADS, HEAD_DIM),
                                       dv.reshape(Bn, S, SW_KV_HEADS, HEAD_DIM), sinks)
    branches = jnp.einsum('bsgc,gcd->bsgd', jnp.stack([ya, yb, yc, yd], axis=2), w_branch)
    gates = jax.nn.sigmoid(gl.reshape(Bn, S, N_BRANCH, D_MODEL))
    mixed = jnp.sum(gates * branches, axis=2) @ w_o
    x = x + gate1 * mixed
    h2 = rmsnorm(x, g_norm2) * (1.0 + scale2) + shift2
    return x + gate2 * peer_ffn(h2, w_q, sub_keys, u_experts, v_experts)


def setup_inputs(seed: int = 0) -> dict:
    key = jax.random.key(seed)
    ks = jax.random.split(key, 24)
    L, D = DEPTH, D_MODEL

    def nrm(k, shape, scale):
        return jax.random.normal(k, shape, F32) * scale

    return {
        'x': nrm(ks[0], (BATCH, SEQ, D), 1.0),
        'c': nrm(ks[1], (BATCH, D), 1.0),
        'w_ada': nrm(ks[2], (L, D, 6 * D), 0.5 * D ** -0.5),
        'b_ada': nrm(ks[3], (L, 6 * D), 0.01),
        'g_norm1': 1.0 + nrm(ks[4], (L, D), 0.02),
        'w_in': nrm(ks[5], (L, D, IN_COLS), D ** -0.5),
        'lam_q1': nrm(ks[6], (L, HEAD_DIM), 0.1),
        'lam_k1': nrm(ks[7], (L, HEAD_DIM), 0.1),
        'lam_q2': nrm(ks[8], (L, HEAD_DIM), 0.1),
        'lam_k2': nrm(ks[9], (L, HEAD_DIM), 0.1),
        'g_subln': 1.0 + nrm(ks[10], (L, 2 * HEAD_DIM), 0.02),
        'conv_w': nrm(ks[11], (L, CONV_W, BRANCH_W), CONV_W ** -0.5),
        'w_pool': nrm(ks[12], (L, N_POOL, POOL_GROUP, POOL_GROUP), POOL_GROUP ** -0.5),
        'pool_scale': 1.0 + nrm(ks[13], (L, BRANCH_W), 0.1),
        'sinks': nrm(ks[14], (L, SW_Q_HEADS), 1.0),
        'w_branch': nrm(ks[15], (L, N_BRANCH, BRANCH_W, D), BRANCH_W ** -0.5),
        'w_o': nrm(ks[16], (L, D, D), D ** -0.5),
        'g_norm2': 1.0 + nrm(ks[17], (L, D), 0.02),
        'w_q': nrm(ks[18], (L, D, PEER_HEADS * PEER_DKEY), D ** -0.5),
        'sub_keys': nrm(ks[19], (L, 2, PEER_NKEYS, PEER_DKEY // 2), (PEER_DKEY // 2) ** -0.5),
        'u_experts': nrm(ks[20], (L, PEER_EXPERTS, D), D ** -0.5),
        'v_experts': nrm(ks[21], (L, PEER_EXPERTS, D), 1.0),
        'g_final': 1.0 + nrm(ks[22], (D,), 0.02),
    }


def reference(x, c, w_ada, b_ada, g_norm1, w_in, lam_q1, lam_k1, lam_q2, lam_k2, g_subln,
              conv_w, w_pool, pool_scale, sinks, w_branch, w_o, g_norm2, w_q, sub_keys,
              u_experts, v_experts, g_final):
    c_act = jax.nn.silu(c)
    for l in range(DEPTH):
        mod = c_act @ w_ada[l] + b_ada[l]
        lam_init = 0.8 - 0.6 * math.exp(-0.3 * l)
        x = hybrid_layer(x, mod, g_norm1[l], w_in[l], lam_q1[l], lam_k1[l], lam_q2[l], lam_k2[l],
                         g_subln[l], conv_w[l], w_pool[l], pool_scale[l], sinks[l], w_branch[l],
                         w_o[l], g_norm2[l], w_q[l], sub_keys[l], u_experts[l], v_experts[l],
                         lam_init)
    return rmsnorm(x, g_final)
```

```python
import functools
import math

import jax
import jax.numpy as jnp
from jax import lax
from jax.experimental import pallas as pl
from jax.experimental.pallas import tpu as pltpu

F32 = jnp.float32
BF16 = jnp.bfloat16
EPS = 1e-6
NEG_INF = -1e30
LOWEST = -3.0e38

HEAD_DIM = 64
LANES = 128
SUBLANES = 8
MXU_N = 256
N_BRANCH = 4
DA_HEADS = 4
SW_Q_HEADS = 8
SW_KV_HEADS = 2
SW_BLOCK = 128
PEER_HEADS = 8
PEER_NKEYS = 128
PEER_TOPK = 16
VMEM_LIMIT = 56 * 1024 * 1024

TM_IN = 512
TQ_A = 256
TM_MERGE = 256
TM_ROUTE = 256
TM_PEER = 512
NI_PEER = SUBLANES
TE_PEER = NI_PEER * PEER_NKEYS


def _cparams(*sem):
    return pltpu.CompilerParams(dimension_semantics=sem, vmem_limit_bytes=VMEM_LIMIT)


def _rms_mod(x, g, scale, shift):
    y = x * lax.rsqrt(jnp.mean(x * x, axis=-1, keepdims=True) + EPS)
    return (y * g) * (1.0 + scale) + shift


def _dot_nt(a, b):
    return lax.dot_general(a, b, (((1,), (1,)), ((), ())), preferred_element_type=F32)


def _ada_kernel(c_ref, w_ref, b_ref, o_ref):
    c = c_ref[...]
    ca = c * jax.nn.sigmoid(c)
    o_ref[0] = jnp.dot(ca, w_ref[0], preferred_element_type=F32,
                       precision=lax.Precision.HIGHEST) + b_ref[0]


def _ada(c, w_ada, b_ada):
    L, D, N = w_ada.shape
    B = c.shape[0]
    tn = 1536
    return pl.pallas_call(
        _ada_kernel,
        grid=(L, N // tn),
        in_specs=[pl.BlockSpec((B, D), lambda l, j: (0, 0)),
                  pl.BlockSpec((1, D, tn), lambda l, j: (l, 0, j)),
                  pl.BlockSpec((1, 1, tn), lambda l, j: (l, 0, j))],
        out_specs=pl.BlockSpec((1, B, tn), lambda l, j: (l, 0, j)),
        out_shape=jax.ShapeDtypeStruct((L, B, N), F32),
        compiler_params=_cparams("parallel", "parallel"),
        name="ada",
    )(c, w_ada, b_ada.reshape(L, 1, N))


def _in_proj_kernel(x_ref, mod_ref, g_ref, w_ref, o_ref, h_scr):
    @pl.when(pl.program_id(1) == 0)
    def _():
        m = mod_ref[0]
        h = _rms_mod(x_ref[...], g_ref[...], m[1:2], m[0:1])
        h_scr[...] = h.astype(BF16)

    o_ref[...] = jnp.dot(h_scr[...], w_ref[...],
                         preferred_element_type=F32).astype(o_ref.dtype)


def _in_proj(x2, mod, g, w, out_dtype, tn, seq):
    T, D = x2.shape
    N = w.shape[1]
    tm = TM_IN
    per_b = seq // tm
    return pl.pallas_call(
        _in_proj_kernel,
        grid=(T // tm, N // tn),
        in_specs=[pl.BlockSpec((tm, D), lambda i, j: (i, 0)),
                  pl.BlockSpec((1, 6, D), lambda i, j: (i // per_b, 0, 0)),
                  pl.BlockSpec((1, D), lambda i, j: (0, 0)),
                  pl.BlockSpec((D, tn), lambda i, j: (0, j))],
        out_specs=pl.BlockSpec((tm, tn), lambda i, j: (i, j)),
        out_shape=jax.ShapeDtypeStruct((T, N), out_dtype),
        scratch_shapes=[pltpu.VMEM((tm, D), BF16)],
        compiler_params=_cparams("parallel", "arbitrary"),
        name="in_proj",
    )(x2, mod, g, w)


def _attn_a_kernel(lam_ref, g_ref, q_ref, k_ref, v_ref, o_ref, *, lam_init, tq):
    qi = pl.program_id(2)
    lv = lam_ref[...]
    lam = (jnp.exp(jnp.sum(lv[0:1] * lv[1:2], axis=-1, keepdims=True))
           - jnp.exp(jnp.sum(lv[2:3] * lv[3:4], axis=-1, keepdims=True)) + lam_init)

    q = q_ref[...] * jnp.asarray(HEAD_DIM ** -0.5, BF16)
    lane = lax.broadcasted_iota(jnp.int32, q.shape, 1)
    zero = jnp.zeros_like(q)
    qh = (jnp.where(lane < HEAD_DIM, q, zero), jnp.where(lane >= HEAD_DIM, q, zero))

    def step(k, v, carry, mask):
        out = []
        for c in range(2):
            m, l, a = carry[c]
            s = _dot_nt(qh[c], k)
            if mask is not None:
                s = jnp.where(mask, s, NEG_INF)
            m_new = jnp.maximum(m, jnp.max(s, axis=-1, keepdims=True))
            alpha = jnp.exp(m - m_new)
            p = jnp.exp(s - m_new)
            l = alpha * l + jnp.sum(p, axis=-1, keepdims=True)
            a = alpha * a + jnp.dot(p.astype(BF16), v, preferred_element_type=F32)
            out.append((m_new, l, a))
        return tuple(out)

    def body(kb, carry):
        start = pl.multiple_of(kb * tq, tq)
        return step(k_ref[pl.ds(start, tq), :], v_ref[pl.ds(start, tq), :], carry, None)

    init1 = (jnp.full((tq, 1), NEG_INF, F32), jnp.zeros((tq, 1), F32),
             jnp.zeros((tq, LANES), F32))
    carry = lax.fori_loop(0, qi, body, (init1, init1))
    row = lax.broadcasted_iota(jnp.int32, (tq, tq), 0)
    col = lax.broadcasted_iota(jnp.int32, (tq, tq), 1)
    dstart = pl.multiple_of(qi * tq, tq)
    (_, l1, a1), (_, l2, a2) = step(k_ref[pl.ds(dstart, tq), :],
                                    v_ref[pl.ds(dstart, tq), :], carry, col <= row)
    o = a1 / l1 - lam * (a2 / l2)
    y = o * lax.rsqrt(jnp.mean(o * o, axis=-1, keepdims=True) + EPS)
    o_ref[...] = ((y * g_ref[...]) * (1.0 - lam_init)).astype(o_ref.dtype)


def _attn_a(pa, lamv, g_subln, lam_init, batch, seq):
    T = pa.shape[0]
    tq = TQ_A
    nq = seq // tq
    kern = functools.partial(_attn_a_kernel, lam_init=lam_init, tq=tq)
    return pl.pallas_call(
        kern,
        grid=(batch, DA_HEADS, nq),
        in_specs=[pl.BlockSpec((4, HEAD_DIM), lambda b, h, i: (0, 0)),
                  pl.BlockSpec((1, LANES), lambda b, h, i: (0, 0)),
                  pl.BlockSpec((tq, LANES), lambda b, h, i: (b * nq + i, h)),
                  pl.BlockSpec((seq, LANES), lambda b, h, i: (b, DA_HEADS + h)),
                  pl.BlockSpec((seq, LANES), lambda b, h, i: (b, 2 * DA_HEADS + h))],
        out_specs=pl.BlockSpec((tq, LANES), lambda b, h, i: (b * nq + i, h)),
        out_shape=jax.ShapeDtypeStruct((T, DA_HEADS * LANES), BF16),
        compiler_params=_cparams("parallel", "parallel", "arbitrary"),
        name="attn_a",
    )(lamv, g_subln, pa, pa, pa)


def _attn_d_kernel(sink_ref, q_ref, kp_ref, kc_ref, vp_ref, vc_ref, o_ref):
    n = pl.program_id(1)
    qb = SW_BLOCK
    kcat = jnp.concatenate([kp_ref[...], kc_ref[...]], axis=0)
    vcat = jnp.concatenate([vp_ref[...], vc_ref[...]], axis=0)
    qi = lax.broadcasted_iota(jnp.int32, (qb, 2 * qb), 0)
    kj = lax.broadcasted_iota(jnp.int32, (qb, 2 * qb), 1)
    rel = qi + qb - kj
    valid = (rel >= 0) & (rel < qb) & ((n > 0) | (kj >= qb))
    lane = lax.broadcasted_iota(jnp.int32, (qb, LANES), 1)
    lo = lane < HEAD_DIM
    src = lax.broadcasted_iota(jnp.int32, (LANES, LANES), 0)
    dst = lax.broadcasted_iota(jnp.int32, (LANES, LANES), 1)
    scale = jnp.asarray(HEAD_DIM ** -0.5, BF16)
    rep = SW_Q_HEADS // SW_KV_HEADS
    for g in range(SW_KV_HEADS):
        dup = (src == g * HEAD_DIM + (dst & (HEAD_DIM - 1))).astype(F32).astype(BF16)
        kd = jnp.dot(kcat, dup, preferred_element_type=F32).astype(BF16)
        vd = jnp.dot(vcat, dup, preferred_element_type=F32).astype(BF16)
        for pr in range(rep // 2):
            c0 = (g * (rep // 2) + pr) * LANES
            qpair = q_ref[:, c0:c0 + LANES] * scale
            zero = jnp.zeros_like(qpair)
            outs = []
            for half in range(2):
                qh = jnp.where(lo, qpair, zero) if half == 0 else jnp.where(lo, zero, qpair)
                s = jnp.where(valid, _dot_nt(qh, kd), NEG_INF)
                sink = sink_ref[g * rep + pr * 2 + half]
                m = jnp.maximum(jnp.max(s, axis=-1, keepdims=True), sink)
                e = jnp.exp(s - m)
                den = jnp.sum(e, axis=-1, keepdims=True) + jnp.exp(sink - m)
                outs.append(jnp.dot(e.astype(BF16), vd, preferred_element_type=F32) / den)
            o_ref[:, c0:c0 + LANES] = jnp.where(lo, outs[0], outs[1]).astype(o_ref.dtype)


def _attn_d(pa, sinks, batch, seq):
    T = pa.shape[0]
    nb = seq // SW_BLOCK
    qcol = (3 * DA_HEADS * LANES) // (SW_Q_HEADS * HEAD_DIM)
    kcol = (3 * DA_HEADS * LANES + SW_Q_HEADS * HEAD_DIM) // LANES

    def cur(c):
        return lambda b, n: (b * nb + n, c)

    def prev(c):
        return lambda b, n: (b * nb + jnp.maximum(n - 1, 0), c)

    return pl.pallas_call(
        _attn_d_kernel,
        grid=(batch, nb),
        in_specs=[pl.BlockSpec(memory_space=pltpu.SMEM),
                  pl.BlockSpec((SW_BLOCK, SW_Q_HEADS * HEAD_DIM), cur(qcol)),
                  pl.BlockSpec((SW_BLOCK, LANES), prev(kcol)),
                  pl.BlockSpec((SW_BLOCK, LANES), cur(kcol)),
                  pl.BlockSpec((SW_BLOCK, LANES), prev(kcol + 1)),
                  pl.BlockSpec((SW_BLOCK, LANES), cur(kcol + 1))],
        out_specs=pl.BlockSpec((SW_BLOCK, SW_Q_HEADS * HEAD_DIM), lambda b, n: (b * nb + n, 0)),
        out_shape=jax.ShapeDtypeStruct((T, SW_Q_HEADS * HEAD_DIM), BF16),
        compiler_params=_cparams("parallel", "arbitrary"),
        name="attn_d",
    )(sinks, pa, pa, pa, pa, pa)


_PAD = 16
_CHUNK = 256


def _convpool_kernel(bb_ref, bc_ref, bx_ref, cz_ref, cw_ref, wp_ref, ps_ref,
                     yb_ref, yc_ref, za, zb, *, seq):
    c = pl.program_id(1)
    nchunk = seq // _CHUNK
    za[0:_PAD, :] = jnp.zeros((_PAD, LANES), F32)
    zb[0:_PAD, :] = jnp.zeros((_PAD, LANES), F32)

    def rows(r, shift=0, pad=0):
        return slice(pad + r * _CHUNK - shift, pad + (r + 1) * _CHUNK - shift)

    for r in range(nchunk):
        za[rows(r, 0, _PAD), :] = bc_ref[rows(r), :] * bx_ref[rows(r), :]
    cw = cw_ref[...]
    for r in range(nchunk):
        y = (cw[0:1] * za[rows(r, 2, _PAD), :] + cw[1:2] * za[rows(r, 1, _PAD), :]
             + cw[2:3] * za[rows(r, 0, _PAD), :])
        yb_ref[rows(r), :] = (bb_ref[rows(r), :] * y).astype(yb_ref.dtype)

    for r in range(nchunk):
        za[rows(r, 0, _PAD), :] = cz_ref[rows(r), :]
    bufs = (za, zb)
    for lvl in range(4):
        src, dst = bufs[lvl % 2], bufs[(lvl + 1) % 2]
        sh = 1 << lvl
        active = lvl <= c
        for r in range(nchunk):
            shifted = src[rows(r, sh, _PAD), :]
            dst[rows(r, 0, _PAD), :] = src[rows(r, 0, _PAD), :] + jnp.where(active, shifted, 0.0)
    win = 2 << c
    wp = wp_ref[0].astype(BF16)
    for r in range(nchunk):
        z = cz_ref[rows(r), :]
        pos = lax.broadcasted_iota(jnp.int32, (_CHUNK, LANES), 0) + r * _CHUNK
        cnt = jnp.minimum(pos + 1, win).astype(F32)
        d = za[rows(r, 0, _PAD), :] / cnt - z
        y = jnp.dot(d.astype(BF16), wp, preferred_element_type=F32) * ps_ref[...]
        yc_ref[rows(r), :] = y.astype(yc_ref.dtype)


def _convpool(pr, conv_w, w_pool, pool_scale, batch, seq):
    T = pr.shape[0]
    ng = w_pool.shape[0]
    kern = functools.partial(_convpool_kernel, seq=seq)

    def col(k):
        return lambda b, c: (b, k * ng + c)

    out = jax.ShapeDtypeStruct((T, ng * LANES), BF16)
    return pl.pallas_call(
        kern,
        grid=(batch, ng),
        in_specs=[pl.BlockSpec((seq, LANES), col(0)),
                  pl.BlockSpec((seq, LANES), col(1)),
                  pl.BlockSpec((seq, LANES), col(2)),
                  pl.BlockSpec((seq, LANES), col(3)),
                  pl.BlockSpec((3, LANES), lambda b, c: (0, c)),
                  pl.BlockSpec((1, LANES, LANES), lambda b, c: (c, 0, 0)),
                  pl.BlockSpec((1, LANES), lambda b, c: (0, c))],
        out_specs=[pl.BlockSpec((seq, LANES), lambda b, c: (b, c)),
                   pl.BlockSpec((seq, LANES), lambda b, c: (b, c))],
        out_shape=[out, out],
        scratch_shapes=[pltpu.VMEM((_PAD + seq, LANES), F32),
                        pltpu.VMEM((_PAD + seq, LANES), F32)],
        compiler_params=_cparams("parallel", "parallel"),
        name="convpool",
    )(pr, pr, pr, pr, conv_w, w_pool, pool_scale)


def _merge_kernel(ya_ref, yb_ref, yc_ref, yd_ref, g0_ref, g1_ref, g2_ref, g3_ref,
                  x_ref, mod_ref, gn_ref, wb_ref, wo_ref, xo_ref, h2_ref):
    ys = (ya_ref, yb_ref, yc_ref, yd_ref)
    gs = (g0_ref, g1_ref, g2_ref, g3_ref)
    mixed = None
    for g in range(N_BRANCH):
        br = jnp.dot(ys[g][...], wb_ref[g], preferred_element_type=F32)
        term = jax.nn.sigmoid(gs[g][...]) * br
        mixed = term if mixed is None else mixed + term
    mixed = jnp.dot(mixed.astype(BF16), wo_ref[...], preferred_element_type=F32)
    m = mod_ref[0]
    xo = x_ref[...] + m[2:3] * mixed
    xo_ref[...] = xo
    h2_ref[...] = _rms_mod(xo, gn_ref[...], m[4:5], m[3:4]).astype(h2_ref.dtype)


def _merge(ya, yb, yc, yd, pr, x2, mod, g_norm2, wb, wo, seq):
    T, D = x2.shape
    tm = TM_MERGE
    per_b = seq // tm
    bw = ya.shape[1]
    gate0 = (4 * bw) // D

    def tok(i):
        return (i, 0)

    def gate(k):
        return lambda i: (i, gate0 + k)

    return pl.pallas_call(
        _merge_kernel,
        grid=(T // tm,),
        in_specs=[pl.BlockSpec((tm, bw), tok)] * 4
        + [pl.BlockSpec((tm, D), gate(k)) for k in range(N_BRANCH)]
        + [pl.BlockSpec((tm, D), tok),
           pl.BlockSpec((1, 6, D), lambda i: (i // per_b, 0, 0)),
           pl.BlockSpec((1, D), lambda i: (0, 0)),
           pl.BlockSpec((N_BRANCH, bw, D), lambda i: (0, 0, 0)),
           pl.BlockSpec((D, D), lambda i: (0, 0))],
        out_specs=[pl.BlockSpec((tm, D), tok), pl.BlockSpec((tm, D), tok)],
        out_shape=[jax.ShapeDtypeStruct((T, D), F32), jax.ShapeDtypeStruct((T, D), BF16)],
        compiler_params=_cparams("parallel"),
        name="merge",
    )(ya, yb, yc, yd, pr, pr, pr, pr, x2, mod, g_norm2, wb, wo)


def _top_rows(vals, count):
    out = []
    for _ in range(count):
        m = jnp.max(vals, axis=0, keepdims=True)
        out.append(m)
        vals = jnp.where(vals == m, LOWEST, vals)
    return out


def _stack_rows(rows_, n):
    k = len(rows_)
    sub = lax.broadcasted_iota(jnp.int32, (k, n), 0)
    acc = jnp.broadcast_to(rows_[0], (k, n))
    for r in range(1, k):
        acc = jnp.where(sub == r, rows_[r], acc)
    return acc


def _route_kernel(h_ref, wq_ref, keys_ref, e2_ref, phi_ref, e1_ref):
    tm = h_ref.shape[0]
    q = jnp.dot(h_ref[...], wq_ref[...], preferred_element_type=F32)
    sub8 = lax.broadcasted_iota(jnp.int32, (SUBLANES, tm), 0)
    k = PEER_TOPK
    for h in range(PEER_HEADS):
        s, top = [], []
        for p in range(2):
            c0 = (h * 2 + p) * PEER_NKEYS
            qhp = q[:, c0:c0 + PEER_NKEYS].astype(BF16)
            sp = _dot_nt(keys_ref[p], qhp)
            s.append(sp)
            top.append(_top_rows(sp, k + 1))
        m1, m2 = top[0][0], top[1][0]
        t1 = _stack_rows(top[0][:k], tm)
        t2 = _stack_rows(top[1][:k], tm)
        cand = [t1 + top[1][0]]
        for b in range(1, 8):
            amax = (k + 1) // (b + 1)
            blk = jnp.where(sub8 < amax, t1[0:SUBLANES] + top[1][b], LOWEST)
            if b == 7:
                blk = jnp.where(sub8 == 2, top[0][k] + m2, blk)
                blk = jnp.where(sub8 == 3, m1 + top[1][k], blk)
            cand.append(blk)
        cand.append(t2[SUBLANES:2 * SUBLANES] + m1)
        csort = _top_rows(jnp.concatenate(cand, axis=0), k + 1)
        c0_ = csort[0]
        z = jnp.ones_like(c0_)
        for r in range(1, k):
            z = z + jnp.exp(csort[r] - c0_)
        tau = 0.5 * (csort[k - 1] + csort[k])
        e2_ref[h] = jnp.exp(s[1] - m2)
        phi_ref[h] = jnp.exp((tau - m2) - s[0])
        e1_ref[h] = jnp.exp(s[0] - m1) / z


def _route(h2, wq, keys):
    T, D = h2.shape
    tm = TM_ROUTE
    nq = wq.shape[1]
    out = jax.ShapeDtypeStruct((PEER_HEADS, PEER_NKEYS, T), F32)
    ospec = pl.BlockSpec((PEER_HEADS, PEER_NKEYS, tm), lambda i: (0, 0, i))
    return pl.pallas_call(
        _route_kernel,
        grid=(T // tm,),
        in_specs=[pl.BlockSpec((tm, D), lambda i: (i, 0)),
                  pl.BlockSpec((D, nq), lambda i: (0, 0)),
                  pl.BlockSpec((2, PEER_NKEYS, PEER_NKEYS), lambda i: (0, 0, 0))],
        out_specs=[ospec, ospec, ospec],
        out_shape=[out, out, out],
        compiler_params=_cparams("parallel"),
        name="route",
    )(h2, wq, keys)


def _peer_kernel(x_ref, mod_ref, gf_ref, h_ref, u_ref, vt_ref, e2_ref, phi_ref, e1_ref,
                 o_ref, acc_ref, a_scr, w_scr, *, final):
    e = pl.program_id(1)
    tm = h_ref.shape[0]

    @pl.when(e == 0)
    def _():
        acc_ref[...] = jnp.zeros_like(acc_ref)

    sqrt_half = 0.7071067811865476
    for sb in range(tm // MXU_N):
        cols = slice(sb * MXU_N, (sb + 1) * MXU_N)
        a_scr[...] = _dot_nt(u_ref[...], h_ref[cols, :])
        for tb in range(MXU_N // LANES):
            lanes = slice(sb * MXU_N + tb * LANES, sb * MXU_N + (tb + 1) * LANES)
            alanes = slice(tb * LANES, (tb + 1) * LANES)
            for il in range(NI_PEER):
                gsum = jnp.zeros((PEER_NKEYS, LANES), F32)
                for h in range(PEER_HEADS):
                    e2 = e2_ref[h, :, lanes]
                    phi = phi_ref[h, il:il + 1, lanes]
                    e1 = e1_ref[h, il:il + 1, lanes]
                    gsum = gsum + jnp.where(e2 >= phi, e2, 0.0) * e1
                erows = slice(il * PEER_NKEYS, (il + 1) * PEER_NKEYS)
                a = a_scr[erows, alanes]
                gelu = 0.5 * a * (1.0 + lax.erf(a * sqrt_half))
                w_scr[erows, alanes] = (gsum * gelu).astype(BF16)
        acc_ref[:, cols] += jnp.dot(vt_ref[...], w_scr[...], preferred_element_type=F32)

    @pl.when(e == pl.num_programs(1) - 1)
    def _():
        m = mod_ref[0]
        xo = x_ref[...] + m[5:6] * acc_ref[...].T
        if final:
            xo = (xo * lax.rsqrt(jnp.mean(xo * xo, axis=-1, keepdims=True) + EPS)) * gf_ref[...]
        o_ref[...] = xo


def _peer(x2, mod, g_final, h2, u, vt, e2, phi, e1, seq, final):
    T, D = x2.shape
    E = u.shape[0]
    tm, te, ni = TM_PEER, TE_PEER, NI_PEER
    per_b = seq // tm
    kern = functools.partial(_peer_kernel, final=final)
    return pl.pallas_call(
        kern,
        grid=(T // tm, E // te),
        in_specs=[pl.BlockSpec((tm, D), lambda t, e: (t, 0)),
                  pl.BlockSpec((1, 6, D), lambda t, e: (t // per_b, 0, 0)),
                  pl.BlockSpec((1, D), lambda t, e: (0, 0)),
                  pl.BlockSpec((tm, D), lambda t, e: (t, 0)),
                  pl.BlockSpec((te, D), lambda t, e: (e, 0)),
                  pl.BlockSpec((D, te), lambda t, e: (0, e)),
                  pl.BlockSpec((PEER_HEADS, PEER_NKEYS, tm), lambda t, e: (0, 0, t)),
                  pl.BlockSpec((PEER_HEADS, ni, tm), lambda t, e: (0, e, t)),
                  pl.BlockSpec((PEER_HEADS, ni, tm), lambda t, e: (0, e, t))],
        out_specs=pl.BlockSpec((tm, D), lambda t, e: (t, 0)),
        out_shape=jax.ShapeDtypeStruct((T, D), F32),
        scratch_shapes=[pltpu.VMEM((D, tm), F32),
                        pltpu.VMEM((te, MXU_N), F32),
                        pltpu.VMEM((te, MXU_N), BF16)],
        compiler_params=_cparams("parallel", "arbitrary"),
        name="peer",
    )(x2, mod, g_final, h2, u, vt, e2, phi, e1)


def kernel(x, c, w_ada, b_ada, g_norm1, w_in, lam_q1, lam_k1, lam_q2, lam_k2, g_subln, conv_w, w_pool, pool_scale, sinks, w_branch, w_o, g_norm2, w_q, sub_keys, u_experts, v_experts, g_final):
    B, S, D = x.shape
    L = w_ada.shape[0]
    T = B * S
    bw = D // 2
    na = 3 * bw
    nbc = 4 * bw
    nd = bw + 2 * SW_KV_HEADS * HEAD_DIM

    mods = _ada(c, w_ada, b_ada).reshape(L, B, 6, D)
    x2 = x.reshape(T, D)
    for l in range(L):
        lam_init = 0.8 - 0.6 * math.exp(-0.3 * l)
        wl = w_in[l].astype(BF16)
        w_attn = jnp.concatenate([wl[:, :na], wl[:, na + nbc:na + nbc + nd]], axis=1)
        w_rest = jnp.concatenate([wl[:, na:na + nbc], wl[:, na + nbc + nd:]], axis=1)
        g1 = g_norm1[l].reshape(1, D)
        pa = _in_proj(x2, mods[l], g1, w_attn, BF16, w_attn.shape[1] // 3, S)
        pr = _in_proj(x2, mods[l], g1, w_rest, F32, 1024, S)
        lamv = jnp.stack([lam_q1[l], lam_k1[l], lam_q2[l], lam_k2[l]])
        ya = _attn_a(pa, lamv, g_subln[l].reshape(1, 2 * HEAD_DIM), lam_init, B, S)
        yd = _attn_d(pa, sinks[l], B, S)
        yb, yc = _convpool(pr, conv_w[l], w_pool[l], pool_scale[l].reshape(1, bw), B, S)
        x2, h2 = _merge(ya, yb, yc, yd, pr, x2, mods[l], g_norm2[l].reshape(1, D),
                        w_branch[l].astype(BF16), w_o[l].astype(BF16), S)
        e2, phi, e1 = _route(h2, w_q[l].astype(BF16), sub_keys[l].astype(BF16))
        x2 = _peer(x2, mods[l], g_final.reshape(1, D), h2, u_experts[l].astype(BF16),
                   v_experts[l].astype(BF16).T, e2, phi, e1, S, final=(l == L - 1))
    return x2.reshape(B, S, D)
```

```python
import functools
import math

import jax
import jax.numpy as jnp
from jax import lax
from jax.experimental import pallas as pl
from jax.experimental.pallas import tpu as pltpu

F32 = jnp.float32
BF16 = jnp.bfloat16
EPS = 1e-6
NEG_INF = -1e30
LOWEST = -3.0e38

HEAD_DIM = 64
LANES = 128
SUBLANES = 8
MXU_N = 256
N_BRANCH = 4
DA_HEADS = 4
SW_Q_HEADS = 8
SW_KV_HEADS = 2
SW_BLOCK = 128
PEER_HEADS = 8
PEER_NKEYS = 128
PEER_TOPK = 16
VMEM_LIMIT = 56 * 1024 * 1024

TM_IN = 512
TQ_A = 256
TM_MERGE = 256
TM_ROUTE = 256
TM_PEER = 512
NI_PEER = SUBLANES
TE_PEER = NI_PEER * PEER_NKEYS


def _cparams(*sem):
    return pltpu.CompilerParams(dimension_semantics=sem, vmem_limit_bytes=VMEM_LIMIT)


def _rms_mod(x, g, scale, shift):
    y = x * lax.rsqrt(jnp.mean(x * x, axis=-1, keepdims=True) + EPS)
    return (y * g) * (1.0 + scale) + shift


def _dot_nt(a, b):
    return lax.dot_general(a, b, (((1,), (1,)), ((), ())), preferred_element_type=F32)


def _ada_kernel(c_ref, w_ref, b_ref, o_ref):
    c = c_ref[...]
    ca = c * jax.nn.sigmoid(c)
    o_ref[0] = jnp.dot(ca, w_ref[0], preferred_element_type=F32,
                       precision=lax.Precision.HIGHEST) + b_ref[0]


def _ada(c, w_ada, b_ada):
    L, D, N = w_ada.shape
    B = c.shape[0]
    tn = 1536
    return pl.pallas_call(
        _ada_kernel,
        grid=(L, N // tn),
        in_specs=[pl.BlockSpec((B, D), lambda l, j: (0, 0)),
                  pl.BlockSpec((1, D, tn), lambda l, j: (l, 0, j)),
                  pl.BlockSpec((1, 1, tn), lambda l, j: (l, 0, j))],
        out_specs=pl.BlockSpec((1, B, tn), lambda l, j: (l, 0, j)),
        out_shape=jax.ShapeDtypeStruct((L, B, N), F32),
        compiler_params=_cparams("parallel", "parallel"),
        name="ada",
    )(c, w_ada, b_ada.reshape(L, 1, N))


def _in_proj_kernel(x_ref, mod_ref, g_ref, w_ref, o_ref, h_scr):
    @pl.when(pl.program_id(1) == 0)
    def _():
        m = mod_ref[0]
        h = _rms_mod(x_ref[...], g_ref[...], m[1:2], m[0:1])
        h_scr[...] = h.astype(BF16)

    o_ref[...] = jnp.dot(h_scr[...], w_ref[...],
                         preferred_element_type=F32).astype(o_ref.dtype)


def _in_proj(x2, mod, g, w, out_dtype, tn, seq):
    T, D = x2.shape
    N = w.shape[1]
    tm = TM_IN
    per_b = seq // tm
    return pl.pallas_call(
        _in_proj_kernel,
        grid=(T // tm, N // tn),
        in_specs=[pl.BlockSpec((tm, D), lambda i, j: (i, 0)),
                  pl.BlockSpec((1, 6, D), lambda i, j: (i // per_b, 0, 0)),
                  pl.BlockSpec((1, D), lambda i, j: (0, 0)),
                  pl.BlockSpec((D, tn), lambda i, j: (0, j))],
        out_specs=pl.BlockSpec((tm, tn), lambda i, j: (i, j)),
        out_shape=jax.ShapeDtypeStruct((T, N), out_dtype),
        scratch_shapes=[pltpu.VMEM((tm, D), BF16)],
        compiler_params=_cparams("parallel", "arbitrary"),
        name="in_proj",
    )(x2, mod, g, w)


def _attn_a_kernel(lam_ref, g_ref, q_ref, k_ref, v_ref, o_ref, *, lam_init, tq):
    qi = pl.program_id(2)
    lv = lam_ref[...]
    lam = (jnp.exp(jnp.sum(lv[0:1] * lv[1:2], axis=-1, keepdims=True))
           - jnp.exp(jnp.sum(lv[2:3] * lv[3:4], axis=-1, keepdims=True)) + lam_init)

    q = q_ref[...] * jnp.asarray(HEAD_DIM ** -0.5, BF16)
    lane = lax.broadcasted_iota(jnp.int32, q.shape, 1)
    zero = jnp.zeros_like(q)
    qh = (jnp.where(lane < HEAD_DIM, q, zero), jnp.where(lane >= HEAD_DIM, q, zero))

    def step(k, v, carry, mask):
        out = []
        for c in range(2):
            m, l, a = carry[c]
            s = _dot_nt(qh[c], k)
            if mask is not None:
                s = jnp.where(mask, s, NEG_INF)
            m_new = jnp.maximum(m, jnp.max(s, axis=-1, keepdims=True))
            alpha = jnp.exp(m - m_new)
            p = jnp.exp(s - m_new)
            l = alpha * l + jnp.sum(p, axis=-1, keepdims=True)
            a = alpha * a + jnp.dot(p.astype(BF16), v, preferred_element_type=F32)
            out.append((m_new, l, a))
        return tuple(out)

    def body(kb, carry):
        start = pl.multiple_of(kb * tq, tq)
        return step(k_ref[pl.ds(start, tq), :], v_ref[pl.ds(start, tq), :], carry, None)

    init1 = (jnp.full((tq, 1), NEG_INF, F32), jnp.zeros((tq, 1), F32),
             jnp.zeros((tq, LANES), F32))
    carry = lax.fori_loop(0, qi, body, (init1, init1))
    row = lax.broadcasted_iota(jnp.int32, (tq, tq), 0)
    col = lax.broadcasted_iota(jnp.int32, (tq, tq), 1)
    dstart = pl.multiple_of(qi * tq, tq)
    (_, l1, a1), (_, l2, a2) = step(k_ref[pl.ds(dstart, tq), :],
                                    v_ref[pl.ds(dstart, tq), :], carry, col <= row)
    o = a1 / l1 - lam * (a2 / l2)
    y = o * lax.rsqrt(jnp.mean(o * o, axis=-1, keepdims=True) + EPS)
    o_ref[...] = ((y * g_ref[...]) * (1.0 - lam_init)).astype(o_ref.dtype)


def _attn_a(pa, lamv, g_subln, lam_init, batch, seq):
    T = pa.shape[0]
    tq = TQ_A
    nq = seq // tq
    kern = functools.partial(_attn_a_kernel, lam_init=lam_init, tq=tq)
    return pl.pallas_call(
        kern,
        grid=(batch, DA_HEADS, nq),
        in_specs=[pl.BlockSpec((4, HEAD_DIM), lambda b, h, i: (0, 0)),
                  pl.BlockSpec((1, LANES), lambda b, h, i: (0, 0)),
                  pl.BlockSpec((tq, LANES), lambda b, h, i: (b * nq + i, h)),
                  pl.BlockSpec((seq, LANES), lambda b, h, i: (b, DA_HEADS + h)),
                  pl.BlockSpec((seq, LANES), lambda b, h, i: (b, 2 * DA_HEADS + h))],
        out_specs=pl.BlockSpec((tq, LANES), lambda b, h, i: (b * nq + i, h)),
        out_shape=jax.ShapeDtypeStruct((T, DA_HEADS * LANES), BF16),
        compiler_params=_cparams("parallel", "parallel", "arbitrary"),
        name="attn_a",
    )(lamv, g_subln, pa, pa, pa)


def _attn_d_kernel(sink_ref, q_ref, kp_ref, kc_ref, vp_ref, vc_ref, o_ref):
    n = pl.program_id(1)
    qb = SW_BLOCK
    kcat = jnp.concatenate([kp_ref[...], kc_ref[...]], axis=0)
    vcat = jnp.concatenate([vp_ref[...], vc_ref[...]], axis=0)
    qi = lax.broadcasted_iota(jnp.int32, (qb, 2 * qb), 0)
    kj = lax.broadcasted_iota(jnp.int32, (qb, 2 * qb), 1)
    rel = qi + qb - kj
    valid = (rel >= 0) & (rel < qb) & ((n > 0) | (kj >= qb))
    lane = lax.broadcasted_iota(jnp.int32, (qb, LANES), 1)
    lo = lane < HEAD_DIM
    src = lax.broadcasted_iota(jnp.int32, (LANES, LANES), 0)
    dst = lax.broadcasted_iota(jnp.int32, (LANES, LANES), 1)
    scale = jnp.asarray(HEAD_DIM ** -0.5, BF16)
    rep = SW_Q_HEADS // SW_KV_HEADS
    for g in range(SW_KV_HEADS):
        dup = (src == g * HEAD_DIM + (dst & (HEAD_DIM - 1))).astype(F32).astype(BF16)
        kd = jnp.dot(kcat, dup, preferred_element_type=F32).astype(BF16)
        vd = jnp.dot(vcat, dup, preferred_element_type=F32).astype(BF16)
        for pr in range(rep // 2):
            c0 = (g * (rep // 2) + pr) * LANES
            qpair = q_ref[:, c0:c0 + LANES] * scale
            zero = jnp.zeros_like(qpair)
            outs = []
            for half in range(2):
                qh = jnp.where(lo, qpair, zero) if half == 0 else jnp.where(lo, zero, qpair)
                s = jnp.where(valid, _dot_nt(qh, kd), NEG_INF)
                sink = sink_ref[g * rep + pr * 2 + half]
                m = jnp.maximum(jnp.max(s, axis=-1, keepdims=True), sink)
                e = jnp.exp(s - m)
                den = jnp.sum(e, axis=-1, keepdims=True) + jnp.exp(sink - m)
                outs.append(jnp.dot(e.astype(BF16), vd, preferred_element_type=F32) / den)
            o_ref[:, c0:c0 + LANES] = jnp.where(lo, outs[0], outs[1]).astype(o_ref.dtype)


def _attn_d(pa, sinks, batch, seq):
    T = pa.shape[0]
    nb = seq // SW_BLOCK
    qcol = (3 * DA_HEADS * LANES) // (SW_Q_HEADS * HEAD_DIM)
    kcol = (3 * DA_HEADS * LANES + SW_Q_HEADS * HEAD_DIM) // LANES

    def cur(c):
        return lambda b, n: (b * nb + n, c)

    def prev(c):
        return lambda b, n: (b * nb + jnp.maximum(n - 1, 0), c)

    return pl.pallas_call(
        _attn_d_kernel,
        grid=(batch, nb),
        in_specs=[pl.BlockSpec(memory_space=pltpu.SMEM),
                  pl.BlockSpec((SW_BLOCK, SW_Q_HEADS * HEAD_DIM), cur(qcol)),
                  pl.BlockSpec((SW_BLOCK, LANES), prev(kcol)),
                  pl.BlockSpec((SW_BLOCK, LANES), cur(kcol)),
                  pl.BlockSpec((SW_BLOCK, LANES), prev(kcol + 1)),
                  pl.BlockSpec((SW_BLOCK, LANES), cur(kcol + 1))],
        out_specs=pl.BlockSpec((SW_BLOCK, SW_Q_HEADS * HEAD_DIM), lambda b, n: (b * nb + n, 0)),
        out_shape=jax.ShapeDtypeStruct((T, SW_Q_HEADS * HEAD_DIM), BF16),
        compiler_params=_cparams("parallel", "arbitrary"),
        name="attn_d",
    )(sinks, pa, pa, pa, pa, pa)


_PAD = 16
_CHUNK = 256


def _convpool_kernel(bb_ref, bc_ref, bx_ref, cz_ref, cw_ref, wp_ref, ps_ref,
                     yb_ref, yc_ref, za, zb, *, seq):
    c = pl.program_id(1)
    nchunk = seq // _CHUNK
    za[0:_PAD, :] = jnp.zeros((_PAD, LANES), F32)
    zb[0:_PAD, :] = jnp.zeros((_PAD, LANES), F32)

    def rows(r, shift=0, pad=0):
        return slice(pad + r * _CHUNK - shift, pad + (r + 1) * _CHUNK - shift)

    for r in range(nchunk):
        za[rows(r, 0, _PAD), :] = bc_ref[rows(r), :] * bx_ref[rows(r), :]
    cw = cw_ref[...]
    for r in range(nchunk):
        y = (cw[0:1] * za[rows(r, 2, _PAD), :] + cw[1:2] * za[rows(r, 1, _PAD), :]
             + cw[2:3] * za[rows(r, 0, _PAD), :])
        yb_ref[rows(r), :] = (bb_ref[rows(r), :] * y).astype(yb_ref.dtype)

    for r in range(nchunk):
        za[rows(r, 0, _PAD), :] = cz_ref[rows(r), :]
    bufs = (za, zb)
    for lvl in range(4):
        src, dst = bufs[lvl % 2], bufs[(lvl + 1) % 2]
        sh = 1 << lvl
        active = lvl <= c
        for r in range(nchunk):
            shifted = src[rows(r, sh, _PAD), :]
            dst[rows(r, 0, _PAD), :] = src[rows(r, 0, _PAD), :] + jnp.where(active, shifted, 0.0)
    win = 2 << c
    wp = wp_ref[0].astype(BF16)
    for r in range(nchunk):
        z = cz_ref[rows(r), :]
        pos = lax.broadcasted_iota(jnp.int32, (_CHUNK, LANES), 0) + r * _CHUNK
        cnt = jnp.minimum(pos + 1, win).astype(F32)
        d = za[rows(r, 0, _PAD), :] / cnt - z
        y = jnp.dot(d.astype(BF16), wp, preferred_element_type=F32) * ps_ref[...]
        yc_ref[rows(r), :] = y.astype(yc_ref.dtype)


def _convpool(pr, conv_w, w_pool, pool_scale, batch, seq):
    T = pr.shape[0]
    ng = w_pool.shape[0]
    kern = functools.partial(_convpool_kernel, seq=seq)

    def col(k):
        return lambda b, c: (b, k * ng + c)

    out = jax.ShapeDtypeStruct((T, ng * LANES), BF16)
    return pl.pallas_call(
        kern,
        grid=(batch, ng),
        in_specs=[pl.BlockSpec((seq, LANES), col(0)),
                  pl.BlockSpec((seq, LANES), col(1)),
                  pl.BlockSpec((seq, LANES), col(2)),
                  pl.BlockSpec((seq, LANES), col(3)),
                  pl.BlockSpec((3, LANES), lambda b, c: (0, c)),
                  pl.BlockSpec((1, LANES, LANES), lambda b, c: (c, 0, 0)),
                  pl.BlockSpec((1, LANES), lambda b, c: (0, c))],
        out_specs=[pl.BlockSpec((seq, LANES), lambda b, c: (b, c)),
                   pl.BlockSpec((seq, LANES), lambda b, c: (b, c))],
        out_shape=[out, out],
        scratch_shapes=[pltpu.VMEM((_PAD + seq, LANES), F32),
                        pltpu.VMEM((_PAD + seq, LANES), F32)],
        compiler_params=_cparams("parallel", "parallel"),
        name="convpool",
    )(pr, pr, pr, pr, conv_w, w_pool, pool_scale)


def _merge_kernel(ya_ref, yb_ref, yc_ref, yd_ref, g0_ref, g1_ref, g2_ref, g3_ref,
                  x_ref, mod_ref, gn_ref, wb_ref, wo_ref, xo_ref, h2_ref, h2t_ref):
    ys = (ya_ref, yb_ref, yc_ref, yd_ref)
    gs = (g0_ref, g1_ref, g2_ref, g3_ref)
    mixed = None
    for g in range(N_BRANCH):
        br = jnp.dot(ys[g][...], wb_ref[g], preferred_element_type=F32)
        term = jax.nn.sigmoid(gs[g][...]) * br
        mixed = term if mixed is None else mixed + term
    mixed = jnp.dot(mixed.astype(BF16), wo_ref[...], preferred_element_type=F32)
    m = mod_ref[0]
    xo = x_ref[...] + m[2:3] * mixed
    xo_ref[...] = xo
    h2 = _rms_mod(xo, gn_ref[...], m[4:5], m[3:4])
    h2_ref[...] = h2.astype(h2_ref.dtype)
    h2t_ref[...] = h2.T.astype(h2t_ref.dtype)


def _merge(ya, yb, yc, yd, pr, x2, mod, g_norm2, wb, wo, seq):
    T, D = x2.shape
    tm = TM_MERGE
    per_b = seq // tm
    bw = ya.shape[1]
    gate0 = (4 * bw) // D

    def tok(i):
        return (i, 0)

    def gate(k):
        return lambda i: (i, gate0 + k)

    return pl.pallas_call(
        _merge_kernel,
        grid=(T // tm,),
        in_specs=[pl.BlockSpec((tm, bw), tok)] * 4
        + [pl.BlockSpec((tm, D), gate(k)) for k in range(N_BRANCH)]
        + [pl.BlockSpec((tm, D), tok),
           pl.BlockSpec((1, 6, D), lambda i: (i // per_b, 0, 0)),
           pl.BlockSpec((1, D), lambda i: (0, 0)),
           pl.BlockSpec((N_BRANCH, bw, D), lambda i: (0, 0, 0)),
           pl.BlockSpec((D, D), lambda i: (0, 0))],
        out_specs=[pl.BlockSpec((tm, D), tok), pl.BlockSpec((tm, D), tok),
                   pl.BlockSpec((D, tm), lambda i: (0, i))],
        out_shape=[jax.ShapeDtypeStruct((T, D), F32), jax.ShapeDtypeStruct((T, D), BF16),
                   jax.ShapeDtypeStruct((D, T), BF16)],
        compiler_params=_cparams("parallel"),
        name="merge",
    )(ya, yb, yc, yd, pr, pr, pr, pr, x2, mod, g_norm2, wb, wo)


def _top_rows(vals, count):
    out = []
    for _ in range(count):
        m = jnp.max(vals, axis=0, keepdims=True)
        out.append(m)
        vals = jnp.where(vals == m, LOWEST, vals)
    return out


def _stack_rows(rows_, n):
    k = len(rows_)
    sub = lax.broadcasted_iota(jnp.int32, (k, n), 0)
    acc = jnp.broadcast_to(rows_[0], (k, n))
    for r in range(1, k):
        acc = jnp.where(sub == r, rows_[r], acc)
    return acc


def _route_kernel(h_ref, wq_ref, keys_ref, e2_ref, phi_ref, e1_ref):
    tm = h_ref.shape[0]
    q = jnp.dot(h_ref[...], wq_ref[...], preferred_element_type=F32)
    sub8 = lax.broadcasted_iota(jnp.int32, (SUBLANES, tm), 0)
    k = PEER_TOPK
    for h in range(PEER_HEADS):
        s, top = [], []
        for p in range(2):
            c0 = (h * 2 + p) * PEER_NKEYS
            qhp = q[:, c0:c0 + PEER_NKEYS].astype(BF16)
            sp = _dot_nt(keys_ref[p], qhp)
            s.append(sp)
            top.append(_top_rows(sp, k + 1))
        m1, m2 = top[0][0], top[1][0]
        t1 = _stack_rows(top[0][:k], tm)
        t2 = _stack_rows(top[1][:k], tm)
        cand = [t1 + top[1][0]]
        for b in range(1, 8):
            amax = (k + 1) // (b + 1)
            blk = jnp.where(sub8 < amax, t1[0:SUBLANES] + top[1][b], LOWEST)
            if b == 7:
                blk = jnp.where(sub8 == 2, top[0][k] + m2, blk)
                blk = jnp.where(sub8 == 3, m1 + top[1][k], blk)
            cand.append(blk)
        cand.append(t2[SUBLANES:2 * SUBLANES] + m1)
        csort = _top_rows(jnp.concatenate(cand, axis=0), k + 1)
        c0_ = csort[0]
        z = jnp.ones_like(c0_)
        for r in range(1, k):
            z = z + jnp.exp(csort[r] - c0_)
        tau = 0.5 * (csort[k - 1] + csort[k])
        e2_ref[h] = jnp.exp(s[1] - m2)
        phi_ref[h] = jnp.exp((tau - m2) - s[0])
        e1_ref[h] = jnp.exp(s[0] - m1) / z


def _route(h2, wq, keys):
    T, D = h2.shape
    tm = TM_ROUTE
    nq = wq.shape[1]
    out = jax.ShapeDtypeStruct((PEER_HEADS, PEER_NKEYS, T), F32)
    ospec = pl.BlockSpec((PEER_HEADS, PEER_NKEYS, tm), lambda i: (0, 0, i))
    return pl.pallas_call(
        _route_kernel,
        grid=(T // tm,),
        in_specs=[pl.BlockSpec((tm, D), lambda i: (i, 0)),
                  pl.BlockSpec((D, nq), lambda i: (0, 0)),
                  pl.BlockSpec((2, PEER_NKEYS, PEER_NKEYS), lambda i: (0, 0, 0))],
        out_specs=[ospec, ospec, ospec],
        out_shape=[out, out, out],
        compiler_params=_cparams("parallel"),
        name="route",
    )(h2, wq, keys)


def _peer_kernel(x_ref, mod_ref, gf_ref, ht_ref, u_ref, vt_ref, e2_ref, phi_ref, e1_ref,
                 o_ref, acc_ref, ga0, ga1, w0, w1, *, final):
    e = pl.program_id(1)
    tm = ht_ref.shape[1]

    @pl.when(e == 0)
    def _():
        acc_ref[...] = jnp.zeros_like(acc_ref)
        ga1[...] = jnp.zeros_like(ga1)
        w0[...] = jnp.zeros_like(w0)

    sqrt_half = 0.7071067811865476

    def tick(ga_new, ga_old, w_new, w_old):
        for sb in range(tm // MXU_N):
            cols = slice(sb * MXU_N, (sb + 1) * MXU_N)
            a = jnp.dot(u_ref[...], ht_ref[:, cols], preferred_element_type=F32)
            ga_new[:, cols] = 0.5 * a * (1.0 + lax.erf(a * sqrt_half))
        for tb in range(tm // LANES):
            lanes = slice(tb * LANES, (tb + 1) * LANES)
            for il in range(NI_PEER):
                gsum = jnp.zeros((PEER_NKEYS, LANES), F32)
                for h in range(PEER_HEADS):
                    e2 = e2_ref[h, :, lanes]
                    phi = phi_ref[h, il:il + 1, lanes]
                    e1 = e1_ref[h, il:il + 1, lanes]
                    gsum = gsum + jnp.where(e2 >= phi, e2, 0.0) * e1
                erows = slice(il * PEER_NKEYS, (il + 1) * PEER_NKEYS)
                w_new[erows, lanes] = (gsum * ga_old[erows, lanes]).astype(BF16)
        for sb in range(tm // MXU_N):
            cols = slice(sb * MXU_N, (sb + 1) * MXU_N)
            acc_ref[:, cols] += jnp.dot(vt_ref[...], w_old[:, cols], preferred_element_type=F32)

    @pl.when(e % 2 == 0)
    def _():
        tick(ga0, ga1, w1, w0)

    @pl.when(e % 2 == 1)
    def _():
        tick(ga1, ga0, w0, w1)

    @pl.when(e == pl.num_programs(1) - 1)
    def _():
        m = mod_ref[0]
        xo = x_ref[...] + m[5:6] * acc_ref[...].T
        if final:
            xo = (xo * lax.rsqrt(jnp.mean(xo * xo, axis=-1, keepdims=True) + EPS)) * gf_ref[...]
        o_ref[...] = xo


def _peer(x2, mod, g_final, h2t, u, vt, e2, phi, e1, seq, final):
    T, D = x2.shape
    E = u.shape[0]
    tm, te, ni = TM_PEER, TE_PEER, NI_PEER
    per_b = seq // tm
    ntile = E // te
    kern = functools.partial(_peer_kernel, final=final)

    def tile(off):
        return lambda e: jnp.clip(e + off, 0, ntile - 1)

    ta, tb, tc = tile(0), tile(-1), tile(-2)
    return pl.pallas_call(
        kern,
        grid=(T // tm, ntile + 2),
        in_specs=[pl.BlockSpec((tm, D), lambda t, e: (t, 0)),
                  pl.BlockSpec((1, 6, D), lambda t, e: (t // per_b, 0, 0)),
                  pl.BlockSpec((1, D), lambda t, e: (0, 0)),
                  pl.BlockSpec((D, tm), lambda t, e: (0, t)),
                  pl.BlockSpec((te, D), lambda t, e: (ta(e), 0)),
                  pl.BlockSpec((D, te), lambda t, e: (0, tc(e))),
                  pl.BlockSpec((PEER_HEADS, PEER_NKEYS, tm), lambda t, e: (0, 0, t)),
                  pl.BlockSpec((PEER_HEADS, ni, tm), lambda t, e: (0, tb(e), t)),
                  pl.BlockSpec((PEER_HEADS, ni, tm), lambda t, e: (0, tb(e), t))],
        out_specs=pl.BlockSpec((tm, D), lambda t, e: (t, 0)),
        out_shape=jax.ShapeDtypeStruct((T, D), F32),
        scratch_shapes=[pltpu.VMEM((D, tm), F32),
                        pltpu.VMEM((te, tm), F32), pltpu.VMEM((te, tm), F32),
                        pltpu.VMEM((te, tm), BF16), pltpu.VMEM((te, tm), BF16)],
        compiler_params=_cparams("parallel", "arbitrary"),
        name="peer",
    )(x2, mod, g_final, h2t, u, vt, e2, phi, e1)


def kernel(x, c, w_ada, b_ada, g_norm1, w_in, lam_q1, lam_k1, lam_q2, lam_k2, g_subln, conv_w, w_pool, pool_scale, sinks, w_branch, w_o, g_norm2, w_q, sub_keys, u_experts, v_experts, g_final):
    B, S, D = x.shape
    L = w_ada.shape[0]
    T = B * S
    bw = D // 2
    na = 3 * bw
    nbc = 4 * bw
    nd = bw + 2 * SW_KV_HEADS * HEAD_DIM

    mods = _ada(c, w_ada, b_ada).reshape(L, B, 6, D)
    x2 = x.reshape(T, D)
    for l in range(L):
        lam_init = 0.8 - 0.6 * math.exp(-0.3 * l)
        wl = w_in[l].astype(BF16)
        w_attn = jnp.concatenate([wl[:, :na], wl[:, na + nbc:na + nbc + nd]], axis=1)
        w_rest = jnp.concatenate([wl[:, na:na + nbc], wl[:, na + nbc + nd:]], axis=1)
        g1 = g_norm1[l].reshape(1, D)
        pa = _in_proj(x2, mods[l], g1, w_attn, BF16, w_attn.shape[1] // 3, S)
        pr = _in_proj(x2, mods[l], g1, w_rest, F32, 1024, S)
        lamv = jnp.stack([lam_q1[l], lam_k1[l], lam_q2[l], lam_k2[l]])
        ya = _attn_a(pa, lamv, g_subln[l].reshape(1, 2 * HEAD_DIM), lam_init, B, S)
        yd = _attn_d(pa, sinks[l], B, S)
        yb, yc = _convpool(pr, conv_w[l], w_pool[l], pool_scale[l].reshape(1, bw), B, S)
        x2, h2, h2t = _merge(ya, yb, yc, yd, pr, x2, mods[l], g_norm2[l].reshape(1, D),
                             w_branch[l].astype(BF16), w_o[l].astype(BF16), S)
        e2, phi, e1 = _route(h2, w_q[l].astype(BF16), sub_keys[l].astype(BF16))
        x2 = _peer(x2, mods[l], g_final.reshape(1, D), h2t, u_experts[l].astype(BF16),
                   v_experts[l].astype(BF16).T, e2, phi, e1, S, final=(l == L - 1))
    return x2.reshape(B, S, D)
```
